```python
import jax, jax.numpy as jnp
from jax import lax
import numpy as np

D_MODEL = 2048
BATCH = 4
SEQ = 4096
DEPTH = 4

NORM_EPS = 1e-6
CHUNK = 128
A_EXPAND = 2
A_WIDTH = A_EXPAND * D_MODEL
A_GROUPS = 16
A_GROUP_DIM = A_WIDTH // A_GROUPS
B_WINDOWS = (128, 512, 2048)
B_DILATIONS = (1, 4, 16)
B_NGROUPS = len(B_WINDOWS)
B_HEAD_DIM = 128
B_HEADS = D_MODEL // B_HEAD_DIM
B_WIDTH = B_HEADS * B_HEAD_DIM
B_TOTAL_HEADS = B_NGROUPS * B_HEADS
Q_BLOCK = 128
NEG_INF = -1e30
N_A_LAYERS = (DEPTH + 1) // 2
N_B_LAYERS = DEPTH // 2

kernel_name = "hybrid_gmlp_dilated_swa_alibi"


def rms_norm(x, g):
    x32 = x.astype(jnp.float32)
    y = x32 * lax.rsqrt(jnp.mean(x32 * x32, axis=-1, keepdims=True) + NORM_EPS)
    return (y * g.astype(jnp.float32)).astype(x.dtype)


def layer_norm(x, g, b):
    x32 = x.astype(jnp.float32)
    mu = jnp.mean(x32, axis=-1, keepdims=True)
    var = jnp.mean(jnp.square(x32 - mu), axis=-1, keepdims=True)
    y = (x32 - mu) * lax.rsqrt(var + NORM_EPS)
    return (y * g.astype(jnp.float32) + b.astype(jnp.float32)).astype(x.dtype)


def alibi_slopes():
    n = np.arange(1, B_TOTAL_HEADS + 1, dtype=np.float32)
    return jnp.asarray(np.power(np.float32(2.0), -8.0 * n / B_TOTAL_HEADS).astype(np.float32))


def mixer_a(h, w_in, ln_g, ln_b, w_s, b_s, w_out):
    b, s, _ = h.shape
    p = h @ w_in
    uv = jax.nn.gelu(p[..., :2 * A_WIDTH], approximate=False)
    u, v = uv[..., :A_WIDTH], uv[..., A_WIDTH:]
    z = p[..., 2 * A_WIDTH:]
    v = layer_norm(v, ln_g, ln_b)
    vc = v.reshape(b, s // CHUNK, CHUNK, A_GROUPS, A_GROUP_DIM)
    causal = jnp.tril(jnp.ones((CHUNK, CHUNK), dtype=w_s.dtype))
    ws = w_s * causal[None]
    mixed = jnp.einsum('gts,bnsgc->bntgc', ws, vc) + b_s.T[None, None, :, :, None]
    gated = u * mixed.reshape(b, s, A_WIDTH)
    return (gated * jax.nn.silu(z)) @ w_out


def dilated_window_attention(q, k, v, window, dilation, slopes):
    b, s, h, dh = q.shape
    L = s // dilation
    bd = b * dilation

    def to_sub(t):
        return t.reshape(b, L, dilation, h, dh).transpose(0, 2, 1, 3, 4).reshape(bd, L, h, dh)

    qs, ks, vs = to_sub(q), to_sub(k), to_sub(v)
    span = window // dilation
    n_prev = -(-span // Q_BLOCK)
    nb = -(-L // Q_BLOCK)
    pad = nb * Q_BLOCK - L
    qs = jnp.pad(qs, ((0, 0), (0, pad), (0, 0), (0, 0)))
    ks = jnp.pad(ks, ((0, 0), (n_prev * Q_BLOCK, pad), (0, 0), (0, 0)))
    vs = jnp.pad(vs, ((0, 0), (n_prev * Q_BLOCK, pad), (0, 0), (0, 0)))
    qb = qs.reshape(bd, nb, Q_BLOCK, h, dh)
    kb = ks.reshape(bd, nb + n_prev, Q_BLOCK, h, dh)
    vb = vs.reshape(bd, nb + n_prev, Q_BLOCK, h, dh)
    kw = jnp.concatenate([kb[:, j:j + nb] for j in range(n_prev + 1)], axis=2)
    vw = jnp.concatenate([vb[:, j:j + nb] for j in range(n_prev + 1)], axis=2)
    kwid = (n_prev + 1) * Q_BLOCK

    scale = 1.0 / np.sqrt(dh)
    scores = jnp.einsum('bnqhd,bnkhd->bnhqk', qb, kw,
                        preferred_element_type=jnp.float32) * scale
    qi = jnp.arange(Q_BLOCK)[:, None]
    kc = jnp.arange(kwid)[None, :]
    dist = qi + n_prev * Q_BLOCK - kc
    key_idx = jnp.arange(nb)[:, None, None] * Q_BLOCK - n_prev * Q_BLOCK + kc[None]
    valid = (dist >= 0)[None] & (dist <= span)[None] & (key_idx >= 0)
    bias = -slopes[:, None, None] * (dist * dilation).astype(jnp.float32)[None]
    scores = jnp.where(valid[None, :, None], scores + bias[None, None], NEG_INF)
    m = jnp.max(scores, axis=-1, keepdims=True)
    e = jnp.exp(scores - m)
    den = jnp.sum(e, axis=-1)
    o = jnp.einsum('bnhqk,bnkhd->bnqhd', e, vw.astype(jnp.float32))
    o = o / den.transpose(0, 1, 3, 2)[..., None]
    lse = (m[..., 0] + jnp.log(den)).transpose(0, 1, 3, 2)

    o = o.reshape(bd, nb * Q_BLOCK, h, dh)[:, :L]
    lse = lse.reshape(bd, nb * Q_BLOCK, h)[:, :L]
    o = o.reshape(b, dilation, L, h, dh).transpose(0, 2, 1, 3, 4).reshape(b, s, h, dh)
    lse = lse.reshape(b, dilation, L, h).transpose(0, 2, 1, 3).reshape(b, s, h)
    return o, lse


def mixer_b(h, w_in, w_out, slopes):
    b, s, _ = h.shape
    p = h @ w_in
    outs, lses = [], []
    for g in range(B_NGROUPS):
        off = 3 * g * B_WIDTH
        q = p[..., off:off + B_WIDTH].reshape(b, s, B_HEADS, B_HEAD_DIM)
        k = p[..., off + B_WIDTH:off + 2 * B_WIDTH].reshape(b, s, B_HEADS, B_HEAD_DIM)
        v = p[..., off + 2 * B_WIDTH:off + 3 * B_WIDTH].reshape(b, s, B_HEADS, B_HEAD_DIM)
        o, lse = dilated_window_attention(q, k, v, B_WINDOWS[g], B_DILATIONS[g],
                                          slopes[g * B_HEADS:(g + 1) * B_HEADS])
        outs.append(o)
        lses.append(lse)
    wts = jax.nn.softmax(jnp.stack(lses, axis=0), axis=0)
    o = jnp.sum(wts[..., None] * jnp.stack(outs, axis=0), axis=0)
    o = o.reshape(b, s, B_WIDTH).astype(h.dtype)
    z = p[..., 3 * B_NGROUPS * B_WIDTH:]
    return (o * jax.nn.silu(z)) @ w_out


def setup_inputs(seed: int = 0) -> dict:
    key = jax.random.key(seed)
    ks = jax.random.split(key, 12)
    f32 = jnp.float32
    nrm = lambda k, shape: jax.random.normal(k, shape, dtype=f32)
    return {
        "x": nrm(ks[0], (BATCH, SEQ, D_MODEL)),
        "a_norm_g": 1.0 + 0.02 * nrm(ks[1], (N_A_LAYERS, D_MODEL)),
        "a_w_in": nrm(ks[2], (N_A_LAYERS, D_MODEL, 3 * A_WIDTH)) * D_MODEL ** -0.5,
        "a_ln_g": 1.0 + 0.02 * nrm(ks[3], (N_A_LAYERS, A_WIDTH)),
        "a_ln_b": 0.02 * nrm(ks[4], (N_A_LAYERS, A_WIDTH)),
        "a_w_s": nrm(ks[5], (N_A_LAYERS, A_GROUPS, CHUNK, CHUNK)) * CHUNK ** -0.5,
        "a_b_s": 1.0 + 0.02 * nrm(ks[6], (N_A_LAYERS, A_GROUPS, CHUNK)),
        "a_w_out": nrm(ks[7], (N_A_LAYERS, A_WIDTH, D_MODEL)) * A_WIDTH ** -0.5,
        "b_norm_g": 1.0 + 0.02 * nrm(ks[8], (N_B_LAYERS, D_MODEL)),
        "b_w_in": nrm(ks[9], (N_B_LAYERS, D_MODEL, 3 * B_NGROUPS * B_WIDTH + B_WIDTH)) * D_MODEL ** -0.5,
        "b_w_out": nrm(ks[10], (N_B_LAYERS, B_WIDTH, D_MODEL)) * B_WIDTH ** -0.5,
        "final_norm_g": 1.0 + 0.02 * nrm(ks[11], (D_MODEL,)),
    }


def reference(x, a_norm_g, a_w_in, a_ln_g, a_ln_b, a_w_s, a_b_s, a_w_out,
              b_norm_g, b_w_in, b_w_out, final_norm_g):
    slopes = alibi_slopes()
    for i in range(DEPTH):
        j = i // 2
        if i % 2 == 0:
            h = rms_norm(x, a_norm_g[j])
            x = x + mixer_a(h, a_w_in[j], a_ln_g[j], a_ln_b[j], a_w_s[j], a_b_s[j], a_w_out[j])
        else:
            h = rms_norm(x, b_norm_g[j])
            x = x + mixer_b(h, b_w_in[j], b_w_out[j], slopes)
    return rms_norm(x, final_norm_g)
```

```python
import functools

import numpy as np
import jax
import jax.numpy as jnp
from jax import lax
from jax.experimental import pallas as pl
from jax.experimental.pallas import tpu as pltpu

D_MODEL = 2048
BATCH = 4
SEQ = 4096
DEPTH = 4
NORM_EPS = 1e-6
CHUNK = 128
A_WIDTH = 2 * D_MODEL
A_GROUPS = 16
A_GROUP_DIM = A_WIDTH // A_GROUPS
B_WINDOWS = (128, 512, 2048)
B_DILATIONS = (1, 4, 16)
B_NGROUPS = 3
B_HEAD_DIM = 128
B_HEADS = D_MODEL // B_HEAD_DIM
B_WIDTH = B_HEADS * B_HEAD_DIM
B_TOTAL_HEADS = B_NGROUPS * B_HEADS
B_IN_WIDTH = 3 * B_NGROUPS * B_WIDTH + B_WIDTH
Q_BLOCK = 128
NEG_INF = -1e30
M_TOKENS = BATCH * SEQ

V7X_VMEM_BYTES = 64 * 1024 * 1024
VMEM_LIMIT_BYTES = V7X_VMEM_BYTES - 8 * 1024 * 1024

_SQRT_HALF = 0.7071067811865476


def _alibi_slopes():
    n = np.arange(1, B_TOTAL_HEADS + 1, dtype=np.float32)
    return np.power(np.float32(2.0), -8.0 * n / B_TOTAL_HEADS).astype(np.float32)


def _gelu(x):
    return 0.5 * x * (1.0 + lax.erf(x * _SQRT_HALF))


def _silu(x):
    return x * (1.0 / (1.0 + jnp.exp(-x)))


def _rms_norm_rows(x, g):
    ms = jnp.mean(x * x, axis=-1, keepdims=True)
    return x * lax.rsqrt(ms + NORM_EPS) * g


def _norm_matmul_kernel(x_ref, g_ref, w_ref, o_ref, h_ref, *, n_gelu_tiles):
    j = pl.program_id(1)

    @pl.when(j == 0)
    def _():
        h_ref[...] = _rms_norm_rows(x_ref[...], g_ref[...]).astype(jnp.bfloat16)

    acc = jnp.dot(h_ref[...], w_ref[...], preferred_element_type=jnp.float32)
    if n_gelu_tiles == 0:
        o_ref[...] = acc.astype(o_ref.dtype)
    else:
        @pl.when(j < n_gelu_tiles)
        def _():
            o_ref[...] = _gelu(acc).astype(o_ref.dtype)

        @pl.when(j >= n_gelu_tiles)
        def _():
            o_ref[...] = acc.astype(o_ref.dtype)


def _norm_matmul(x, g, w, *, n_gelu_cols, tm=1024, tn=1024):
    m, d = x.shape
    n = w.shape[1]
    kern = functools.partial(_norm_matmul_kernel, n_gelu_tiles=n_gelu_cols // tn)
    return pl.pallas_call(
        kern,
        grid=(m // tm, n // tn),
        in_specs=[
            pl.BlockSpec((tm, d), lambda i, j: (i, 0)),
            pl.BlockSpec((1, d), lambda i, j: (0, 0)),
            pl.BlockSpec((d, tn), lambda i, j: (0, j)),
        ],
        out_specs=pl.BlockSpec((tm, tn), lambda i, j: (i, j)),
        out_shape=jax.ShapeDtypeStruct((m, n), jnp.bfloat16),
        scratch_shapes=[pltpu.VMEM((tm, d), jnp.bfloat16)],
        compiler_params=pltpu.CompilerParams(
            dimension_semantics=("parallel", "arbitrary"),
            vmem_limit_bytes=VMEM_LIMIT_BYTES),
        name="norm_matmul",
    )(x, g, w)


def _gmlp_out_kernel(u_ref, v_ref, z_ref, lng_ref, lnb_ref, ws_ref, bst_ref, w_ref, x_ref,
                     o_ref, y_ref, gt_ref, *, tm):
    v = v_ref[...].astype(jnp.float32)
    mu = jnp.mean(v, axis=-1, keepdims=True)
    vc = v - mu
    var = jnp.mean(vc * vc, axis=-1, keepdims=True)
    y = vc * lax.rsqrt(var + NORM_EPS) * lng_ref[...] + lnb_ref[...]
    y_ref[...] = y.astype(jnp.bfloat16)

    row = lax.broadcasted_iota(jnp.int32, (CHUNK, CHUNK), 0)
    col = lax.broadcasted_iota(jnp.int32, (CHUNK, CHUNK), 1)
    causal = col <= row
    for g in range(A_GROUPS):
        ws = jnp.where(causal, ws_ref[g], 0.0).astype(jnp.bfloat16)
        bias = bst_ref[:, g:g + 1]
        cs = slice(g * A_GROUP_DIM, (g + 1) * A_GROUP_DIM)
        for c in range(tm // CHUNK):
            rs = slice(c * CHUNK, (c + 1) * CHUNK)
            mixed = jnp.dot(ws, y_ref[rs, cs], preferred_element_type=jnp.float32) + bias
            gated = u_ref[rs, cs].astype(jnp.float32) * mixed
            gt_ref[rs, cs] = (gated * _silu(z_ref[rs, cs].astype(jnp.float32))).astype(jnp.bfloat16)

    o_ref[...] = x_ref[...] + jnp.dot(gt_ref[...], w_ref[...], preferred_element_type=jnp.float32)


def _gmlp_out(p, ln_g, ln_b, w_s, b_s_t, w_out, x, *, tm=256):
    m, d = x.shape
    kern = functools.partial(_gmlp_out_kernel, tm=tm)
    const = lambda *shape: pl.BlockSpec(shape, lambda i: (0,) * len(shape))
    return pl.pallas_call(
        kern,
        grid=(m // tm,),
        in_specs=[
            pl.BlockSpec((tm, A_WIDTH), lambda i: (i, 0)),
            pl.BlockSpec((tm, A_WIDTH), lambda i: (i, 1)),
            pl.BlockSpec((tm, A_WIDTH), lambda i: (i, 2)),
            const(1, A_WIDTH),
            const(1, A_WIDTH),
            const(A_GROUPS, CHUNK, CHUNK),
            const(CHUNK, A_GROUPS),
            pl.BlockSpec((A_WIDTH, d), lambda i: (0, 0), pipeline_mode=pl.Buffered(1)),
            pl.BlockSpec((tm, d), lambda i: (i, 0)),
        ],
        out_specs=pl.BlockSpec((tm, d), lambda i: (i, 0)),
        out_shape=jax.ShapeDtypeStruct((m, d), jnp.float32),
        scratch_shapes=[pltpu.VMEM((tm, A_WIDTH), jnp.bfloat16),
                        pltpu.VMEM((tm, A_WIDTH), jnp.bfloat16)],
        compiler_params=pltpu.CompilerParams(
            dimension_semantics=("parallel",),
            vmem_limit_bytes=VMEM_LIMIT_BYTES),
        name="gmlp_out",
    )(p, p, p, ln_g, ln_b, w_s, b_s_t, w_out, x)


def _attn_kernel(q_ref, kc_ref, kp_ref, vc_ref, vp_ref, o_ref, lse_ref, *, tq, dilation, slopes):
    i = pl.program_id(2)
    scale = np.float32(1.0 / np.sqrt(B_HEAD_DIM))
    qi = lax.broadcasted_iota(jnp.int32, (Q_BLOCK, Q_BLOCK), 0)
    kj = lax.broadcasted_iota(jnp.int32, (Q_BLOCK, Q_BLOCK), 1)
    dist_d = qi - kj
    dist_p = qi + Q_BLOCK - kj
    valid_d = dist_d >= 0
    valid_p = dist_p <= Q_BLOCK
    valid_p_first = jnp.logical_and(valid_p, i > 0)
    fdist_d = (dist_d * dilation).astype(jnp.float32)
    fdist_p = (dist_p * dilation).astype(jnp.float32)
    lane = lax.broadcasted_iota(jnp.int32, (Q_BLOCK, 128), 1)
    nt = (((1,), (1,)), ((), ()))

    for a in range(tq // Q_BLOCK):
        rs = slice(a * Q_BLOCK, (a + 1) * Q_BLOCK)
        lse_tile = jnp.zeros((Q_BLOCK, 128), jnp.float32)
        for h in range(B_HEADS):
            cs = slice(h * B_HEAD_DIM, (h + 1) * B_HEAD_DIM)
            q = q_ref[0, rs, cs]
            k_d = kc_ref[0, rs, cs]
            v_d = vc_ref[0, rs, cs]
            if a == 0:
                k_p = kp_ref[0, :, cs]
                v_p = vp_ref[0, :, cs]
                vp_mask = valid_p_first
            else:
                ps = slice((a - 1) * Q_BLOCK, a * Q_BLOCK)
                k_p = kc_ref[0, ps, cs]
                v_p = vc_ref[0, ps, cs]
                vp_mask = valid_p
            slope = np.float32(slopes[h])
            s_d = lax.dot_general(q, k_d, nt, preferred_element_type=jnp.float32) * scale
            s_p = lax.dot_general(q, k_p, nt, preferred_element_type=jnp.float32) * scale
            s_d = jnp.where(valid_d, s_d + (-slope) * fdist_d, NEG_INF)
            s_p = jnp.where(vp_mask, s_p + (-slope) * fdist_p, NEG_INF)
            m = jnp.maximum(jnp.max(s_d, axis=-1, keepdims=True), jnp.max(s_p, axis=-1, keepdims=True))
            e_d = jnp.exp(s_d - m)
            e_p = jnp.exp(s_p - m)
            den = jnp.sum(e_d, axis=-1, keepdims=True) + jnp.sum(e_p, axis=-1, keepdims=True)
            acc = (jnp.dot(e_d.astype(jnp.bfloat16), v_d, preferred_element_type=jnp.float32)
                   + jnp.dot(e_p.astype(jnp.bfloat16), v_p, preferred_element_type=jnp.float32))
            o_ref[0, rs, cs] = (acc / den).astype(o_ref.dtype)
            lse_tile = jnp.where(lane == h, m + jnp.log(den), lse_tile)
        lse_ref[0, rs, :] = lse_tile


def _attention_group(p3, g, *, tq=256):
    d = B_DILATIONS[g]
    assert B_WINDOWS[g] // d == Q_BLOCK
    L = SEQ // d
    nblk = B_IN_WIDTH // B_WIDTH
    pv = p3.reshape(BATCH, L, d * B_IN_WIDTH)
    slopes = tuple(float(s) for s in _alibi_slopes()[g * B_HEADS:(g + 1) * B_HEADS])
    kern = functools.partial(_attn_kernel, tq=tq, dilation=d, slopes=slopes)
    sub = tq // Q_BLOCK
    cur = lambda off: pl.BlockSpec((1, tq, B_WIDTH), lambda b, r, i: (b, i, r * nblk + 3 * g + off))
    prev = lambda off: pl.BlockSpec(
        (1, Q_BLOCK, B_WIDTH), lambda b, r, i: (b, jnp.maximum(i * sub - 1, 0), r * nblk + 3 * g + off))
    o, lse = pl.pallas_call(
        kern,
        grid=(BATCH, d, L // tq),
        in_specs=[cur(0), cur(1), prev(1), cur(2), prev(2)],
        out_specs=[pl.BlockSpec((1, tq, B_WIDTH), lambda b, r, i: (b, i, r)),
                   pl.BlockSpec((1, tq, 128), lambda b, r, i: (b, i, r))],
        out_shape=[jax.ShapeDtypeStruct((BATCH, L, d * B_WIDTH), jnp.bfloat16),
                   jax.ShapeDtypeStruct((BATCH, L, d * 128), jnp.float32)],
        compiler_params=pltpu.CompilerParams(
            dimension_semantics=("parallel", "parallel", "arbitrary"),
            vmem_limit_bytes=VMEM_LIMIT_BYTES),
        name=f"dilated_attn_g{g}",
    )(pv, pv, pv, pv, pv)
    return o.reshape(M_TOKENS, B_WIDTH), lse.reshape(M_TOKENS, 128)


def _merge_out_kernel(o0_ref, o1_ref, o2_ref, l0_ref, l1_ref, l2_ref, z_ref, w_ref, x_ref, fg_ref,
                      out_ref, gt_ref, *, final_norm):
    l0, l1, l2 = l0_ref[...], l1_ref[...], l2_ref[...]
    mx = jnp.maximum(jnp.maximum(l0, l1), l2)
    e0, e1, e2 = jnp.exp(l0 - mx), jnp.exp(l1 - mx), jnp.exp(l2 - mx)
    inv = 1.0 / (e0 + e1 + e2)
    w0, w1, w2 = e0 * inv, e1 * inv, e2 * inv
    for h in range(B_HEADS):
        cs = slice(h * B_HEAD_DIM, (h + 1) * B_HEAD_DIM)
        o = (w0[:, h:h + 1] * o0_ref[:, cs].astype(jnp.float32)
             + w1[:, h:h + 1] * o1_ref[:, cs].astype(jnp.float32)
             + w2[:, h:h + 1] * o2_ref[:, cs].astype(jnp.float32))
        gt_ref[:, cs] = (o * _silu(z_ref[:, cs].astype(jnp.float32))).astype(jnp.bfloat16)
    out = x_ref[...] + jnp.dot(gt_ref[...], w_ref[...], preferred_element_type=jnp.float32)
    if final_norm:
        out = _rms_norm_rows(out, fg_ref[...])
    out_ref[...] = out


def _merge_out(os_, lses, p, w_out, x, final_g, *, final_norm, tm=512):
    m, d = x.shape
    z_blk = 3 * B_NGROUPS
    kern = functools.partial(_merge_out_kernel, final_norm=final_norm)
    row = lambda w: pl.BlockSpec((tm, w), lambda i: (i, 0))
    return pl.pallas_call(
        kern,
        grid=(m // tm,),
        in_specs=[row(B_WIDTH), row(B_WIDTH), row(B_WIDTH), row(128), row(128), row(128),
                  pl.BlockSpec((tm, B_WIDTH), lambda i: (i, z_blk)),
                  pl.BlockSpec((B_WIDTH, d), lambda i: (0, 0), pipeline_mode=pl.Buffered(1)),
                  row(d),
                  pl.BlockSpec((1, d), lambda i: (0, 0))],
        out_specs=row(d),
        out_shape=jax.ShapeDtypeStruct((m, d), jnp.float32),
        scratch_shapes=[pltpu.VMEM((tm, B_WIDTH), jnp.bfloat16)],
        compiler_params=pltpu.CompilerParams(
            dimension_semantics=("parallel",),
            vmem_limit_bytes=VMEM_LIMIT_BYTES),
        name="merge_out",
    )(*os_, *lses, p, w_out, x, final_g)


def kernel(x, a_norm_g, a_w_in, a_ln_g, a_ln_b, a_w_s, a_b_s, a_w_out, b_norm_g, b_w_in, b_w_out, final_norm_g):
    bf16 = jnp.bfloat16
    xm = x.reshape(M_TOKENS, D_MODEL)
    fg = final_norm_g.reshape(1, D_MODEL)
    for i in range(DEPTH):
        j = i // 2
        if i % 2 == 0:
            p = _norm_matmul(xm, a_norm_g[j].reshape(1, D_MODEL), a_w_in[j].astype(bf16),
                             n_gelu_cols=2 * A_WIDTH)
            xm = _gmlp_out(p, a_ln_g[j].reshape(1, A_WIDTH), a_ln_b[j].reshape(1, A_WIDTH),
                           a_w_s[j], a_b_s[j].T, a_w_out[j].astype(bf16), xm)
        else:
            p = _norm_matmul(xm, b_norm_g[j].reshape(1, D_MODEL), b_w_in[j].astype(bf16), n_gelu_cols=0)
            p3 = p.reshape(BATCH, SEQ, B_IN_WIDTH)
            outs = [_attention_group(p3, g) for g in range(B_NGROUPS)]
            xm = _merge_out([o for o, _ in outs], [l for _, l in outs], p, b_w_out[j].astype(bf16),
                            xm, fg, final_norm=(i == DEPTH - 1))
    return xm.reshape(BATCH, SEQ, D_MODEL)
```

```python
import functools

import numpy as np
import jax
import jax.numpy as jnp
from jax import lax
from jax.experimental import pallas as pl
from jax.experimental.pallas import tpu as pltpu

D_MODEL = 2048
BATCH = 4
SEQ = 4096
DEPTH = 4
NORM_EPS = 1e-6
CHUNK = 128
A_WIDTH = 2 * D_MODEL
A_GROUPS = 16
A_GROUP_DIM = A_WIDTH // A_GROUPS
B_WINDOWS = (128, 512, 2048)
B_DILATIONS = (1, 4, 16)
B_NGROUPS = 3
B_HEAD_DIM = 128
B_HEADS = D_MODEL // B_HEAD_DIM
B_WIDTH = B_HEADS * B_HEAD_DIM
B_TOTAL_HEADS = B_NGROUPS * B_HEADS
B_IN_WIDTH = 3 * B_NGROUPS * B_WIDTH + B_WIDTH
Q_BLOCK = 128
NEG_INF = -1e30
M_TOKENS = BATCH * SEQ
LANES = 128
PERM_BLOCK = 256

V7X_VMEM_BYTES = 64 * 1024 * 1024
VMEM_LIMIT_BYTES = V7X_VMEM_BYTES - 8 * 1024 * 1024

_SQRT_HALF = 0.7071067811865476

assert DEPTH % 2 == 0 and all(w // d == Q_BLOCK for w, d in zip(B_WINDOWS, B_DILATIONS))


def _alibi_slopes():
    n = np.arange(1, B_TOTAL_HEADS + 1, dtype=np.float32)
    return np.power(np.float32(2.0), -8.0 * n / B_TOTAL_HEADS).astype(np.float32)


def _gelu(x):
    return 0.5 * x * (1.0 + lax.erf(x * _SQRT_HALF))


def _silu(x):
    return x * (1.0 / (1.0 + jnp.exp(-x)))


def _rms_norm_rows(x, g):
    ms = jnp.mean(x * x, axis=-1, keepdims=True)
    return x * lax.rsqrt(ms + NORM_EPS) * g


def _deinterleave_matrix(dilation, transpose=False):
    n = PERM_BLOCK // dilation
    a = lax.broadcasted_iota(jnp.int32, (PERM_BLOCK, PERM_BLOCK), 0)
    b = lax.broadcasted_iota(jnp.int32, (PERM_BLOCK, PERM_BLOCK), 1)
    dst, src = (b, a) if transpose else (a, b)
    hit = src == (dst % n) * dilation + dst // n
    return jnp.where(hit, 1.0, 0.0).astype(jnp.bfloat16)


def _params(*sem):
    return pltpu.CompilerParams(dimension_semantics=sem, vmem_limit_bytes=VMEM_LIMIT_BYTES)


def _norm_kernel(x_ref, g_ref, h_ref):
    h_ref[...] = _rms_norm_rows(x_ref[...], g_ref[...]).astype(h_ref.dtype)


def _norm(x, g, *, tm=512):
    m, d = x.shape
    return pl.pallas_call(
        _norm_kernel,
        grid=(m // tm,),
        in_specs=[pl.BlockSpec((tm, d), lambda i: (i, 0)), pl.BlockSpec((1, d), lambda i: (0, 0))],
        out_specs=pl.BlockSpec((tm, d), lambda i: (i, 0)),
        out_shape=jax.ShapeDtypeStruct((m, d), jnp.bfloat16),
        compiler_params=_params("parallel"),
        name="rms_norm",
    )(x, g)


def _matmul_gelu_kernel(h_ref, w_ref, o_ref, *, n_gelu_tiles):
    j = pl.program_id(1)
    acc = jnp.dot(h_ref[...], w_ref[...], preferred_element_type=jnp.float32)

    @pl.when(j < n_gelu_tiles)
    def _():
        o_ref[...] = _gelu(acc).astype(o_ref.dtype)

    @pl.when(j >= n_gelu_tiles)
    def _():
        o_ref[...] = acc.astype(o_ref.dtype)


def _matmul_gelu(h, w, *, n_gelu_cols, tm=1024, tn=1024):
    m, d = h.shape
    n = w.shape[1]
    kern = functools.partial(_matmul_gelu_kernel, n_gelu_tiles=n_gelu_cols // tn)
    return pl.pallas_call(
        kern,
        grid=(m // tm, n // tn),
        in_specs=[pl.BlockSpec((tm, d), lambda i, j: (i, 0)),
                  pl.BlockSpec((d, tn), lambda i, j: (0, j))],
        out_specs=pl.BlockSpec((tm, tn), lambda i, j: (i, j)),
        out_shape=jax.ShapeDtypeStruct((m, n), jnp.bfloat16),
        compiler_params=_params("parallel", "arbitrary"),
        name="matmul_gelu",
    )(h, w)


def _gmlp_out_kernel(u_ref, v_ref, z_ref, lng_ref, lnb_ref, ws_ref, bst_ref, w_ref, x_ref, ng_ref,
                     o_ref, h_ref, y_ref, gt_ref, *, tm):
    v = v_ref[...].astype(jnp.float32)
    mu = jnp.mean(v, axis=-1, keepdims=True)
    vc = v - mu
    var = jnp.mean(vc * vc, axis=-1, keepdims=True)
    y = vc * lax.rsqrt(var + NORM_EPS) * lng_ref[...] + lnb_ref[...]
    y_ref[...] = y.astype(jnp.bfloat16)

    row = lax.broadcasted_iota(jnp.int32, (CHUNK, CHUNK), 0)
    col = lax.broadcasted_iota(jnp.int32, (CHUNK, CHUNK), 1)
    causal = col <= row
    for g in range(A_GROUPS):
        ws = jnp.where(causal, ws_ref[g], 0.0).astype(jnp.bfloat16)
        bias = bst_ref[:, g:g + 1]
        cs = slice(g * A_GROUP_DIM, (g + 1) * A_GROUP_DIM)
        for c in range(tm // CHUNK):
            rs = slice(c * CHUNK, (c + 1) * CHUNK)
            mixed = jnp.dot(ws, y_ref[rs, cs], preferred_element_type=jnp.float32) + bias
            gated = u_ref[rs, cs].astype(jnp.float32) * mixed
            gt_ref[rs, cs] = (gated * _silu(z_ref[rs, cs].astype(jnp.float32))).astype(jnp.bfloat16)

    out = x_ref[...] + jnp.dot(gt_ref[...], w_ref[...], preferred_element_type=jnp.float32)
    o_ref[...] = out
    h_ref[...] = _rms_norm_rows(out, ng_ref[...]).astype(h_ref.dtype)


def _gmlp_out(p, ln_g, ln_b, w_s, b_s_t, w_out, x, next_g, *, tm=256):
    m, d = x.shape
    kern = functools.partial(_gmlp_out_kernel, tm=tm)
    const = lambda *shape: pl.BlockSpec(shape, lambda i: (0,) * len(shape))
    return pl.pallas_call(
        kern,
        grid=(m // tm,),
        in_specs=[
            pl.BlockSpec((tm, A_WIDTH), lambda i: (i, 0)),
            pl.BlockSpec((tm, A_WIDTH), lambda i: (i, 1)),
            pl.BlockSpec((tm, A_WIDTH), lambda i: (i, 2)),
            const(1, A_WIDTH),
            const(1, A_WIDTH),
            const(A_GROUPS, CHUNK, CHUNK),
            const(CHUNK, A_GROUPS),
            pl.BlockSpec((A_WIDTH, d), lambda i: (0, 0), pipeline_mode=pl.Buffered(1)),
            pl.BlockSpec((tm, d), lambda i: (i, 0)),
            const(1, d),
        ],
        out_specs=[pl.BlockSpec((tm, d), lambda i: (i, 0)), pl.BlockSpec((tm, d), lambda i: (i, 0))],
        out_shape=[jax.ShapeDtypeStruct((m, d), jnp.float32), jax.ShapeDtypeStruct((m, d), jnp.bfloat16)],
        scratch_shapes=[pltpu.VMEM((tm, A_WIDTH), jnp.bfloat16),
                        pltpu.VMEM((tm, A_WIDTH), jnp.bfloat16)],
        compiler_params=_params("parallel"),
        name="gmlp_out",
    )(p, p, p, ln_g, ln_b, w_s, b_s_t, w_out, x, next_g)


_B_TN = 1024
_B_HEADS_PER_TILE = _B_TN // B_HEAD_DIM
_B_GROUP_TILES = 3 * B_WIDTH // _B_TN
_B_Z_TILES = B_WIDTH // _B_TN
_B_Z_START = _B_GROUP_TILES
_B_G1_START = _B_Z_START + _B_Z_TILES
_B_G2_START = _B_G1_START + _B_GROUP_TILES
_B_STEPS = _B_G2_START + _B_GROUP_TILES


def _b_weight_tile(j):
    z_first = 3 * B_NGROUPS * B_WIDTH // _B_TN
    return jnp.where(j < _B_Z_START, j,
                     jnp.where(j < _B_G1_START, j - _B_Z_START + z_first, j - _B_Z_TILES))


def _qkv_in_kernel(h_ref, w_ref, q0_ref, q1_ref, q2_ref, z_ref, hp_ref, *, tm):
    j = pl.program_id(1)

    def permute_rows(dilation):
        perm = _deinterleave_matrix(dilation)
        n = PERM_BLOCK // dilation
        for blk in range(tm // PERM_BLOCK):
            res = jnp.dot(perm, h_ref[blk * PERM_BLOCK:(blk + 1) * PERM_BLOCK, :],
                          preferred_element_type=jnp.float32).astype(jnp.bfloat16)
            for r in range(dilation):
                dst = r * (tm // dilation) + blk * n
                hp_ref[dst:dst + n, :] = res[r * n:(r + 1) * n, :]

    @pl.when(j == _B_G1_START)
    def _():
        permute_rows(B_DILATIONS[1])

    @pl.when(j == _B_G2_START)
    def _():
        permute_rows(B_DILATIONS[2])

    def store_heads(out_ref, acc, dilation):
        for hh in range(_B_HEADS_PER_TILE):
            blk = acc[:, hh * B_HEAD_DIM:(hh + 1) * B_HEAD_DIM].astype(jnp.bfloat16)
            out_ref[hh] = blk.reshape(dilation, tm // dilation, B_HEAD_DIM)

    @pl.when(j < _B_Z_START)
    def _():
        store_heads(q0_ref, jnp.dot(h_ref[...], w_ref[...], preferred_element_type=jnp.float32),
                    B_DILATIONS[0])

    @pl.when(jnp.logical_and(j >= _B_Z_START, j < _B_G1_START))
    def _():
        z_ref[...] = jnp.dot(h_ref[...], w_ref[...], preferred_element_type=jnp.float32).astype(z_ref.dtype)

    @pl.when(jnp.logical_and(j >= _B_G1_START, j < _B_G2_START))
    def _():
        store_heads(q1_ref, jnp.dot(hp_ref[...], w_ref[...], preferred_element_type=jnp.float32),
                    B_DILATIONS[1])

    @pl.when(j >= _B_G2_START)
    def _():
        store_heads(q2_ref, jnp.dot(hp_ref[...], w_ref[...], preferred_element_type=jnp.float32),
                    B_DILATIONS[2])


def _qkv_in(h, w, *, tm=1024):
    m, d = h.shape
    tiles_per_seq = SEQ // tm
    kern = functools.partial(_qkv_in_kernel, tm=tm)

    def qkv_spec(start, dilation):
        return pl.BlockSpec(
            (None, _B_HEADS_PER_TILE, dilation, tm // dilation, B_HEAD_DIM),
            lambda i, j: (i // tiles_per_seq, jnp.clip(j - start, 0, _B_GROUP_TILES - 1), 0,
                          i % tiles_per_seq, 0))

    def qkv_shape(dilation):
        return jax.ShapeDtypeStruct((BATCH, 3 * B_HEADS, dilation, SEQ // dilation, B_HEAD_DIM), jnp.bfloat16)

    return pl.pallas_call(
        kern,
        grid=(m // tm, _B_STEPS),
        in_specs=[pl.BlockSpec((tm, d), lambda i, j: (i, 0)),
                  pl.BlockSpec((d, _B_TN), lambda i, j: (0, _b_weight_tile(j)))],
        out_specs=[qkv_spec(0, B_DILATIONS[0]), qkv_spec(_B_G1_START, B_DILATIONS[1]),
                   qkv_spec(_B_G2_START, B_DILATIONS[2]),
                   pl.BlockSpec((tm, _B_TN), lambda i, j: (i, jnp.clip(j - _B_Z_START, 0, _B_Z_TILES - 1)))],
        out_shape=[qkv_shape(B_DILATIONS[0]), qkv_shape(B_DILATIONS[1]), qkv_shape(B_DILATIONS[2]),
                   jax.ShapeDtypeStruct((m, B_WIDTH), jnp.bfloat16)],
        scratch_shapes=[pltpu.VMEM((tm, d), jnp.bfloat16)],
        compiler_params=_params("parallel", "arbitrary"),
        name="qkv_in",
    )(h, w)


def _attn_kernel(q_ref, kc_ref, kp_ref, vc_ref, vp_ref, bias_ref, o_ref, lse_ref, kw_ref, va_ref, *, tq):
    i = pl.program_id(2)
    scale = np.float32(1.0 / np.sqrt(B_HEAD_DIM))
    kw_ref[:, :Q_BLOCK, :] = kp_ref[...]
    kw_ref[:, Q_BLOCK:, :] = kc_ref[...]
    va_ref[:, :Q_BLOCK, :B_HEAD_DIM] = vp_ref[...]
    va_ref[:, Q_BLOCK:, :B_HEAD_DIM] = vc_ref[...]
    va_ref[:, :, B_HEAD_DIM:] = jnp.ones((B_HEADS, tq + Q_BLOCK, B_HEAD_DIM), jnp.bfloat16)

    qi = lax.broadcasted_iota(jnp.int32, (Q_BLOCK, 2 * Q_BLOCK), 0)
    kc = lax.broadcasted_iota(jnp.int32, (Q_BLOCK, 2 * Q_BLOCK), 1)
    dist = qi + Q_BLOCK - kc
    valid = jnp.logical_and(dist >= 0, dist <= Q_BLOCK)
    valid_first = jnp.logical_and(valid, jnp.logical_or(kc >= Q_BLOCK, i > 0))
    lane = lax.broadcasted_iota(jnp.int32, (Q_BLOCK, LANES), 1)

    for a in range(tq // Q_BLOCK):
        rs = slice(a * Q_BLOCK, (a + 1) * Q_BLOCK)
        ws = slice(a * Q_BLOCK, (a + 2) * Q_BLOCK)
        s = jnp.einsum("hqd,hkd->hqk", q_ref[:, rs, :], kw_ref[:, ws, :],
                       preferred_element_type=jnp.float32)
        ok = valid_first if a == 0 else valid
        t = jnp.where(ok[None], s * scale + bias_ref[...], NEG_INF)
        m = jnp.max(t, axis=-1, keepdims=True)
        e = jnp.exp(t - m).astype(jnp.bfloat16)
        acc = jnp.einsum("hqk,hkd->hqd", e, va_ref[:, ws, :], preferred_element_type=jnp.float32)
        den = acc[:, :, B_HEAD_DIM:]
        o_ref[:, rs, :] = (acc[:, :, :B_HEAD_DIM] / den).astype(o_ref.dtype)
        m_tile = jnp.zeros((Q_BLOCK, LANES), jnp.float32)
        den_tile = jnp.ones((Q_BLOCK, LANES), jnp.float32)
        for h in range(B_HEADS):
            m_tile = jnp.where(lane == h, m[h], m_tile)
            den_tile = jnp.where(lane == h, den[h], den_tile)
        lse_ref[rs, :] = m_tile + jnp.log(den_tile)


def _alibi_bias(g):
    qi = np.arange(Q_BLOCK)[:, None]
    kc = np.arange(2 * Q_BLOCK)[None, :]
    dist = (qi + Q_BLOCK - kc) * B_DILATIONS[g]
    slopes = _alibi_slopes()[g * B_HEADS:(g + 1) * B_HEADS]
    return jnp.asarray(-slopes[:, None, None] * dist.astype(np.float32)[None])


def _attention_group(qkv, g, *, tq=256):
    d = B_DILATIONS[g]
    L = SEQ // d
    sub = tq // Q_BLOCK
    kern = functools.partial(_attn_kernel, tq=tq)
    cur = lambda which: pl.BlockSpec((None, B_HEADS, None, tq, B_HEAD_DIM),
                                     lambda b, r, i: (b, which, r, i, 0))
    prev = lambda which: pl.BlockSpec((None, B_HEADS, None, Q_BLOCK, B_HEAD_DIM),
                                      lambda b, r, i: (b, which, r, jnp.maximum(i * sub - 1, 0), 0))
    return pl.pallas_call(
        kern,
        grid=(BATCH, d, L // tq),
        in_specs=[cur(0), cur(1), prev(1), cur(2), prev(2),
                  pl.BlockSpec((B_HEADS, Q_BLOCK, 2 * Q_BLOCK), lambda b, r, i: (0, 0, 0),
                               pipeline_mode=pl.Buffered(1))],
        out_specs=[pl.BlockSpec((None, B_HEADS, None, tq, B_HEAD_DIM), lambda b, r, i: (b, 0, r, i, 0)),
                   pl.BlockSpec((None, None, tq, LANES), lambda b, r, i: (b, r, i, 0))],
        out_shape=[jax.ShapeDtypeStruct((BATCH, B_HEADS, d, L, B_HEAD_DIM), jnp.bfloat16),
                   jax.ShapeDtypeStruct((BATCH, d, L, LANES), jnp.float32)],
        scratch_shapes=[pltpu.VMEM((B_HEADS, tq + Q_BLOCK, B_HEAD_DIM), jnp.bfloat16),
                        pltpu.VMEM((B_HEADS, tq + Q_BLOCK, 2 * B_HEAD_DIM), jnp.bfloat16)],
        compiler_params=_params("parallel", "parallel", "arbitrary"),
        name=f"dilated_attn_g{g}",
    )(qkv, qkv, qkv, qkv, qkv, _alibi_bias(g))


def _merge_out_kernel(o0_ref, o1_ref, o2_ref, l0_ref, l1_ref, l2_ref, z_ref, w_ref, x_ref, ng_ref,
                      *rest, tm, final_norm):
    if final_norm:
        out_ref, l1s_ref, l2s_ref, gt_ref = rest
    else:
        out_ref, h_ref, l1s_ref, l2s_ref, gt_ref = rest
    d1, d2 = B_DILATIONS[1], B_DILATIONS[2]
    for r in range(d1):
        l1s_ref[pl.ds(r, tm // d1, stride=d1), :] = l1_ref[r]
    for r in range(d2):
        l2s_ref[pl.ds(r, tm // d2, stride=d2), :] = l2_ref[r]
    l0, l1, l2 = l0_ref[...], l1s_ref[...], l2s_ref[...]
    mx = jnp.maximum(jnp.maximum(l0, l1), l2)
    e0, e1, e2 = jnp.exp(l0 - mx), jnp.exp(l1 - mx), jnp.exp(l2 - mx)
    inv = 1.0 / (e0 + e1 + e2)
    w0, w1, w2 = e0 * inv, e1 * inv, e2 * inv

    p1t = _deinterleave_matrix(d1, transpose=True)
    p2t = _deinterleave_matrix(d2, transpose=True)
    for hp in range(B_HEADS // 2):
        pair = (2 * hp, 2 * hp + 1)
        src1 = jnp.concatenate([o1_ref[h].reshape(tm, B_HEAD_DIM) for h in pair], axis=1)
        src2 = jnp.concatenate([o2_ref[h].reshape(tm, B_HEAD_DIM) for h in pair], axis=1)
        t1 = jnp.dot(p1t, src1, preferred_element_type=jnp.float32)
        t2 = jnp.dot(p2t, src2, preferred_element_type=jnp.float32)
        for k, h in enumerate(pair):
            cs = slice(h * B_HEAD_DIM, (h + 1) * B_HEAD_DIM)
            hs = slice(k * B_HEAD_DIM, (k + 1) * B_HEAD_DIM)
            o = (w0[:, h:h + 1] * o0_ref[h].astype(jnp.float32)
                 + w1[:, h:h + 1] * t1[:, hs] + w2[:, h:h + 1] * t2[:, hs])
            gt_ref[:, cs] = (o * _silu(z_ref[:, cs].astype(jnp.float32))).astype(jnp.bfloat16)
    out = x_ref[...] + jnp.dot(gt_ref[...], w_ref[...], preferred_element_type=jnp.float32)
    if final_norm:
        out_ref[...] = _rms_norm_rows(out, ng_ref[...])
    else:
        out_ref[...] = out
        h_ref[...] = _rms_norm_rows(out, ng_ref[...]).astype(h_ref.dtype)


def _merge_out(os_, lses, z, w_out, x, next_g, *, final_norm, tm=PERM_BLOCK):
    m, d = x.shape
    assert tm == PERM_BLOCK
    tiles_per_seq = SEQ // tm
    kern = functools.partial(_merge_out_kernel, tm=tm, final_norm=final_norm)
    row = lambda w: pl.BlockSpec((tm, w), lambda i: (i, 0))

    def o_spec(dil):
        return pl.BlockSpec((None, B_HEADS, dil, tm // dil, B_HEAD_DIM),
                            lambda i: (i // tiles_per_seq, 0, 0, i % tiles_per_seq, 0))

    def l_spec(dil):
        return pl.BlockSpec((None, dil, tm // dil, LANES),
                            lambda i: (i // tiles_per_seq, 0, i % tiles_per_seq, 0))

    out_specs = [row(d)]
    out_shape = [jax.ShapeDtypeStruct((m, d), jnp.float32)]
    if not final_norm:
        out_specs.append(row(d))
        out_shape.append(jax.ShapeDtypeStruct((m, d), jnp.bfloat16))
    return pl.pallas_call(
        kern,
        grid=(m // tm,),
        in_specs=[pl.BlockSpec((None, B_HEADS, None, tm, B_HEAD_DIM),
                               lambda i: (i // tiles_per_seq, 0, 0, i % tiles_per_seq, 0)),
                  o_spec(B_DILATIONS[1]), o_spec(B_DILATIONS[2]),
                  pl.BlockSpec((None, None, tm, LANES), lambda i: (i // tiles_per_seq, 0, i % tiles_per_seq, 0)),
                  l_spec(B_DILATIONS[1]), l_spec(B_DILATIONS[2]),
                  row(B_WIDTH),
                  pl.BlockSpec((B_WIDTH, d), lambda i: (0, 0), pipeline_mode=pl.Buffered(1)),
                  row(d),
                  pl.BlockSpec((1, d), lambda i: (0, 0))],
        out_specs=out_specs,
        out_shape=out_shape,
        scratch_shapes=[pltpu.VMEM((tm, LANES), jnp.float32),
                        pltpu.VMEM((tm, LANES), jnp.float32),
                        pltpu.VMEM((tm, B_WIDTH), jnp.bfloat16)],
        compiler_params=_params("parallel"),
        name="merge_out",
    )(*os_, *lses, z, w_out, x, next_g)


def kernel(x, a_norm_g, a_w_in, a_ln_g, a_ln_b, a_w_s, a_b_s, a_w_out, b_norm_g, b_w_in, b_w_out, final_norm_g):
    bf16 = jnp.bfloat16
    xm = x.reshape(M_TOKENS, D_MODEL)
    gains = []
    for i in range(DEPTH):
        nxt = i + 1
        if nxt == DEPTH:
            gains.append(final_norm_g)
        else:
            gains.append(a_norm_g[nxt // 2] if nxt % 2 == 0 else b_norm_g[nxt // 2])
    gains = [g.reshape(1, D_MODEL) for g in gains]

    h = _norm(xm, a_norm_g[0].reshape(1, D_MODEL))
    for i in range(DEPTH):
        j = i // 2
        if i % 2 == 0:
            p = _matmul_gelu(h, a_w_in[j].astype(bf16), n_gelu_cols=2 * A_WIDTH)
            xm, h = _gmlp_out(p, a_ln_g[j].reshape(1, A_WIDTH), a_ln_b[j].reshape(1, A_WIDTH),
                              a_w_s[j], a_b_s[j].T, a_w_out[j].astype(bf16), xm, gains[i])
        else:
            q0, q1, q2, z = _qkv_in(h, b_w_in[j].astype(bf16))
            outs = [_attention_group(q, g) for g, q in enumerate((q0, q1, q2))]
            res = _merge_out([o for o, _ in outs], [l for _, l in outs], z, b_w_out[j].astype(bf16),
                             xm, gains[i], final_norm=(i == DEPTH - 1))
            if i == DEPTH - 1:
                xm = res[0]
            else:
                xm, h = res
    return xm.reshape(BATCH, SEQ, D_MODEL)
```

```python
import functools

import numpy as np
import jax
import jax.numpy as jnp
from jax import lax
from jax.experimental import pallas as pl
from jax.experimental.pallas import tpu as pltpu

D_MODEL = 2048
BATCH = 4
SEQ = 4096
DEPTH = 4
NORM_EPS = 1e-6
CHUNK = 128
A_WIDTH = 2 * D_MODEL
A_GROUPS = 16
A_GROUP_DIM = A_WIDTH // A_GROUPS
B_WINDOWS = (128, 512, 2048)
B_DILATIONS = (1, 4, 16)
B_NGROUPS = 3
B_HEAD_DIM = 128
B_HEADS = D_MODEL // B_HEAD_DIM
B_WIDTH = B_HEADS * B_HEAD_DIM
B_TOTAL_HEADS = B_NGROUPS * B_HEADS
B_IN_WIDTH = 3 * B_NGROUPS * B_WIDTH + B_WIDTH
Q_BLOCK = 128
NEG_INF = -1e30
M_TOKENS = BATCH * SEQ
LANES = 128
PERM_BLOCK = 256

V7X_VMEM_BYTES = 64 * 1024 * 1024
VMEM_LIMIT_BYTES = V7X_VMEM_BYTES - 8 * 1024 * 1024

TM = 1024
TN = 1024
HEADS_PER_TILE = TN // B_HEAD_DIM

_SQRT_HALF = 0.7071067811865476

assert DEPTH % 2 == 0 and all(w // d == Q_BLOCK for w, d in zip(B_WINDOWS, B_DILATIONS))


def _alibi_slopes():
    n = np.arange(1, B_TOTAL_HEADS + 1, dtype=np.float32)
    return np.power(np.float32(2.0), -8.0 * n / B_TOTAL_HEADS).astype(np.float32)


def _gelu(x):
    return 0.5 * x * (1.0 + lax.erf(x * _SQRT_HALF))


def _silu(x):
    return x * (1.0 / (1.0 + jnp.exp(-x)))


def _rms_norm_rows(x, g):
    ms = jnp.mean(x * x, axis=-1, keepdims=True)
    return x * lax.rsqrt(ms + NORM_EPS) * g


def _deinterleave_matrix(dilation, transpose=False):
    n = PERM_BLOCK // dilation
    a = lax.broadcasted_iota(jnp.int32, (PERM_BLOCK, PERM_BLOCK), 0)
    b = lax.broadcasted_iota(jnp.int32, (PERM_BLOCK, PERM_BLOCK), 1)
    dst, src = (b, a) if transpose else (a, b)
    hit = src == (dst % n) * dilation + dst // n
    return jnp.where(hit, 1.0, 0.0).astype(jnp.bfloat16)


def _params(*sem):
    return pltpu.CompilerParams(dimension_semantics=sem, vmem_limit_bytes=VMEM_LIMIT_BYTES)


def _norm_kernel(x_ref, g_ref, h_ref):
    h_ref[...] = _rms_norm_rows(x_ref[...], g_ref[...]).astype(h_ref.dtype)


def _norm(x, g, *, tm=512):
    m, d = x.shape
    return pl.pallas_call(
        _norm_kernel,
        grid=(m // tm,),
        in_specs=[pl.BlockSpec((tm, d), lambda i: (i, 0)), pl.BlockSpec((1, d), lambda i: (0, 0))],
        out_specs=pl.BlockSpec((tm, d), lambda i: (i, 0)),
        out_shape=jax.ShapeDtypeStruct((m, d), jnp.bfloat16),
        compiler_params=_params("parallel"),
        name="rms_norm",
    )(x, g)


def _proj_kernel(h_ref, w_ref, o_ref, wb_ref, *, gelu_tiles, silu_from, head_major, n_chunks):
    n, m = pl.program_id(0), pl.program_id(1)

    @pl.when(m == 0)
    def _():
        wb_ref[...] = w_ref[...].astype(jnp.bfloat16)

    cw = TN // n_chunks

    def run(act):
        for c in range(n_chunks):
            acc = act(jnp.dot(h_ref[...], wb_ref[:, c * cw:(c + 1) * cw], preferred_element_type=jnp.float32))
            if head_major:
                for hh in range(cw // B_HEAD_DIM):
                    o_ref[c * (cw // B_HEAD_DIM) + hh] = (
                        acc[:, hh * B_HEAD_DIM:(hh + 1) * B_HEAD_DIM].astype(o_ref.dtype))
            else:
                o_ref[:, c * cw:(c + 1) * cw] = acc.astype(o_ref.dtype)

    if gelu_tiles:
        pl.when(n < gelu_tiles)(lambda: run(_gelu))
    if silu_from > gelu_tiles:
        pl.when(jnp.logical_and(n >= gelu_tiles, n < silu_from))(lambda: run(lambda a: a))
    pl.when(n >= silu_from)(lambda: run(_silu))


def _proj(h, w_stack, layer, col_tiles, *, gelu_tiles=0, silu_from, head_major=False, n_chunks=2):
    m, d = h.shape
    nt = len(col_tiles)
    first, split = col_tiles[0], nt
    for k in range(1, nt):
        if col_tiles[k] != col_tiles[k - 1] + 1:
            split = k
            break
    second = col_tiles[split] if split < nt else 0
    assert list(col_tiles) == [first + k for k in range(split)] + [second + k for k in range(nt - split)]

    def w_tile(n):
        return jnp.where(n < split, first + n, second + n - split)

    kern = functools.partial(_proj_kernel, gelu_tiles=gelu_tiles, silu_from=silu_from, head_major=head_major,
                             n_chunks=n_chunks)
    if head_major:
        out_spec = pl.BlockSpec((HEADS_PER_TILE, TM, B_HEAD_DIM), lambda n, i: (n, i, 0))
        out_shape = jax.ShapeDtypeStruct((nt * HEADS_PER_TILE, m, B_HEAD_DIM), jnp.bfloat16)
    else:
        out_spec = pl.BlockSpec((TM, TN), lambda n, i: (i, n))
        out_shape = jax.ShapeDtypeStruct((m, nt * TN), jnp.bfloat16)
    return pl.pallas_call(
        kern,
        grid=(nt, m // TM),
        in_specs=[pl.BlockSpec((TM, d), lambda n, i: (i, 0)),
                  pl.BlockSpec((None, d, TN), lambda n, i: (layer, 0, w_tile(n)))],
        out_specs=out_spec,
        out_shape=out_shape,
        scratch_shapes=[pltpu.VMEM((d, TN), jnp.bfloat16)],
        compiler_params=_params("arbitrary", "arbitrary"),
        name="proj_heads" if head_major else "proj_gelu",
    )(h, w_stack)


def _gmlp_out_kernel(u_ref, v_ref, sz_ref, lng_ref, lnb_ref, ws_ref, bst_ref, w_ref, x_ref, ng_ref,
                     o_ref, h0_ref, h1_ref, h2_ref, y_ref, gt_ref, *, tm):
    v = v_ref[...].astype(jnp.float32)
    mu = jnp.mean(v, axis=-1, keepdims=True)
    vc = v - mu
    var = jnp.mean(vc * vc, axis=-1, keepdims=True)
    y = vc * lax.rsqrt(var + NORM_EPS) * lng_ref[...] + lnb_ref[...]
    y_ref[...] = y.astype(jnp.bfloat16)

    row = lax.broadcasted_iota(jnp.int32, (CHUNK, CHUNK), 0)
    col = lax.broadcasted_iota(jnp.int32, (CHUNK, CHUNK), 1)
    causal = col <= row
    for g in range(A_GROUPS):
        ws = jnp.where(causal, ws_ref[g], 0.0).astype(jnp.bfloat16)
        bias = bst_ref[:, g:g + 1]
        cs = slice(g * A_GROUP_DIM, (g + 1) * A_GROUP_DIM)
        for c in range(tm // CHUNK):
            rs = slice(c * CHUNK, (c + 1) * CHUNK)
            mixed = jnp.dot(ws, y_ref[rs, cs], preferred_element_type=jnp.float32) + bias
            gated = u_ref[rs, cs].astype(jnp.float32) * mixed
            gt_ref[rs, cs] = (gated * sz_ref[rs, cs].astype(jnp.float32)).astype(jnp.bfloat16)

    out = x_ref[...] + jnp.dot(gt_ref[...], w_ref[...], preferred_element_type=jnp.float32)
    o_ref[...] = out
    h = _rms_norm_rows(out, ng_ref[...]).astype(jnp.bfloat16)
    h0_ref[...] = h
    for h_ref, dil in ((h1_ref, B_DILATIONS[1]), (h2_ref, B_DILATIONS[2])):
        hp = jnp.dot(_deinterleave_matrix(dil), h, preferred_element_type=jnp.float32)
        h_ref[...] = hp.astype(jnp.bfloat16).reshape(dil, tm // dil, D_MODEL)


def _gmlp_out(p, ln_g, ln_b, w_s, b_s_t, w_out_stack, layer, x, next_g, *, tm=PERM_BLOCK):
    m, d = x.shape
    assert tm == PERM_BLOCK
    tiles_per_seq = SEQ // tm
    kern = functools.partial(_gmlp_out_kernel, tm=tm)
    const = lambda *shape: pl.BlockSpec(shape, lambda i: (0,) * len(shape))
    row = pl.BlockSpec((tm, d), lambda i: (i, 0))

    def perm_spec(dil):
        return pl.BlockSpec((None, dil, tm // dil, d), lambda i: (i // tiles_per_seq, 0, i % tiles_per_seq, 0))

    def perm_shape(dil):
        return jax.ShapeDtypeStruct((BATCH, dil, SEQ // dil, d), jnp.bfloat16)

    xo, h0, h1, h2 = pl.pallas_call(
        kern,
        grid=(m // tm,),
        in_specs=[
            pl.BlockSpec((tm, A_WIDTH), lambda i: (i, 0)),
            pl.BlockSpec((tm, A_WIDTH), lambda i: (i, 1)),
            pl.BlockSpec((tm, A_WIDTH), lambda i: (i, 2)),
            const(1, A_WIDTH),
            const(1, A_WIDTH),
            const(A_GROUPS, CHUNK, CHUNK),
            const(CHUNK, A_GROUPS),
            pl.BlockSpec((None, A_WIDTH, d), lambda i: (layer, 0, 0), pipeline_mode=pl.Buffered(1)),
            row,
            const(1, d),
        ],
        out_specs=[row, row, perm_spec(B_DILATIONS[1]), perm_spec(B_DILATIONS[2])],
        out_shape=[jax.ShapeDtypeStruct((m, d), jnp.float32), jax.ShapeDtypeStruct((m, d), jnp.bfloat16),
                   perm_shape(B_DILATIONS[1]), perm_shape(B_DILATIONS[2])],
        scratch_shapes=[pltpu.VMEM((tm, A_WIDTH), jnp.bfloat16),
                        pltpu.VMEM((tm, A_WIDTH), jnp.bfloat16)],
        compiler_params=_params("parallel"),
        name="gmlp_out",
    )(p, p, p, ln_g, ln_b, w_s, b_s_t, w_out_stack, x, next_g)
    return xo, (h0, h1.reshape(m, d), h2.reshape(m, d))


def _attn_kernel(q_ref, kc_ref, kp_ref, vc_ref, vp_ref, bias_ref, o_ref, lse_ref, kw_ref, va_ref, *, tq):
    i = pl.program_id(2)
    scale = np.float32(1.0 / np.sqrt(B_HEAD_DIM))
    kw_ref[:, :Q_BLOCK, :] = kp_ref[...]
    kw_ref[:, Q_BLOCK:, :] = kc_ref[...]
    va_ref[:, :Q_BLOCK, :B_HEAD_DIM] = vp_ref[...]
    va_ref[:, Q_BLOCK:, :B_HEAD_DIM] = vc_ref[...]
    va_ref[:, :, B_HEAD_DIM:] = jnp.ones((B_HEADS, tq + Q_BLOCK, B_HEAD_DIM), jnp.bfloat16)

    qi = lax.broadcasted_iota(jnp.int32, (Q_BLOCK, 2 * Q_BLOCK), 0)
    kc = lax.broadcasted_iota(jnp.int32, (Q_BLOCK, 2 * Q_BLOCK), 1)
    dist = qi + Q_BLOCK - kc
    valid = jnp.logical_and(dist >= 0, dist <= Q_BLOCK)
    valid_first = jnp.logical_and(valid, jnp.logical_or(kc >= Q_BLOCK, i > 0))
    lane = lax.broadcasted_iota(jnp.int32, (Q_BLOCK, LANES), 1)

    for a in range(tq // Q_BLOCK):
        rs = slice(a * Q_BLOCK, (a + 1) * Q_BLOCK)
        ws = slice(a * Q_BLOCK, (a + 2) * Q_BLOCK)
        s = jnp.einsum("hqd,hkd->hqk", q_ref[:, rs, :], kw_ref[:, ws, :],
                       preferred_element_type=jnp.float32)
        ok = valid_first if a == 0 else valid
        t = jnp.where(ok[None], s * scale + bias_ref[...], NEG_INF)
        m = jnp.max(t, axis=-1, keepdims=True)
        e = jnp.exp(t - m).astype(jnp.bfloat16)
        acc = jnp.einsum("hqk,hkd->hqd", e, va_ref[:, ws, :], preferred_element_type=jnp.float32)
        den = acc[:, :, B_HEAD_DIM:]
        o_ref[:, rs, :] = (acc[:, :, :B_HEAD_DIM] / den).astype(o_ref.dtype)
        m_tile = jnp.zeros((Q_BLOCK, LANES), jnp.float32)
        den_tile = jnp.ones((Q_BLOCK, LANES), jnp.float32)
        for h in range(B_HEADS):
            m_tile = jnp.where(lane == h, m[h], m_tile)
            den_tile = jnp.where(lane == h, den[h], den_tile)
        lse_ref[rs, :] = m_tile + jnp.log(den_tile)


def _alibi_bias(g):
    qi = np.arange(Q_BLOCK)[:, None]
    kc = np.arange(2 * Q_BLOCK)[None, :]
    dist = (qi + Q_BLOCK - kc) * B_DILATIONS[g]
    slopes = _alibi_slopes()[g * B_HEADS:(g + 1) * B_HEADS]
    return jnp.asarray(-slopes[:, None, None] * dist.astype(np.float32)[None])


def _attention_group(qkv, g):
    d = B_DILATIONS[g]
    L = SEQ // d
    tq = min(512, L)
    sub = tq // Q_BLOCK
    nblk = L // tq
    kern = functools.partial(_attn_kernel, tq=tq)
    cur = lambda which: pl.BlockSpec((B_HEADS, tq, B_HEAD_DIM),
                                     lambda b, r, i: (which, (b * d + r) * nblk + i, 0))
    prev = lambda which: pl.BlockSpec(
        (B_HEADS, Q_BLOCK, B_HEAD_DIM),
        lambda b, r, i: (which, (b * d + r) * (nblk * sub) + jnp.maximum(i * sub - 1, 0), 0))
    return pl.pallas_call(
        kern,
        grid=(BATCH, d, nblk),
        in_specs=[cur(0), cur(1), prev(1), cur(2), prev(2),
                  pl.BlockSpec((B_HEADS, Q_BLOCK, 2 * Q_BLOCK), lambda b, r, i: (0, 0, 0),
                               pipeline_mode=pl.Buffered(1))],
        out_specs=[pl.BlockSpec((B_HEADS, tq, B_HEAD_DIM), lambda b, r, i: (0, (b * d + r) * nblk + i, 0)),
                   pl.BlockSpec((tq, LANES), lambda b, r, i: ((b * d + r) * nblk + i, 0))],
        out_shape=[jax.ShapeDtypeStruct((B_HEADS, M_TOKENS, B_HEAD_DIM), jnp.bfloat16),
                   jax.ShapeDtypeStruct((M_TOKENS, LANES), jnp.float32)],
        scratch_shapes=[pltpu.VMEM((B_HEADS, tq + Q_BLOCK, B_HEAD_DIM), jnp.bfloat16),
                        pltpu.VMEM((B_HEADS, tq + Q_BLOCK, 2 * B_HEAD_DIM), jnp.bfloat16)],
        compiler_params=_params("parallel", "parallel", "arbitrary"),
        name=f"dilated_attn_g{g}",
    )(qkv, qkv, qkv, qkv, qkv, _alibi_bias(g))


def _merge_out_kernel(o0_ref, o1_ref, o2_ref, l0_ref, l1_ref, l2_ref, sz_ref, w_ref, x_ref, ng_ref,
                      *rest, tm, final_norm):
    if final_norm:
        out_ref, l1s_ref, l2s_ref, gt_new_ref, gt_ref = rest
    else:
        out_ref, h_ref, l1s_ref, l2s_ref, gt_new_ref, gt_ref = rest

    @pl.when(pl.program_id(0) == 0)
    def _():
        gt_ref[...] = jnp.zeros(gt_ref.shape, gt_ref.dtype)

    out = x_ref[...] + jnp.dot(gt_ref[...], w_ref[...], preferred_element_type=jnp.float32)
    if final_norm:
        out_ref[...] = _rms_norm_rows(out, ng_ref[...])
    else:
        out_ref[...] = out
        h_ref[...] = _rms_norm_rows(out, ng_ref[...]).astype(h_ref.dtype)

    d1, d2 = B_DILATIONS[1], B_DILATIONS[2]
    for r in range(d1):
        l1s_ref[pl.ds(r, tm // d1, stride=d1), :] = l1_ref[r]
    for r in range(d2):
        l2s_ref[pl.ds(r, tm // d2, stride=d2), :] = l2_ref[r]
    l0, l1, l2 = l0_ref[...], l1s_ref[...], l2s_ref[...]
    mx = jnp.maximum(jnp.maximum(l0, l1), l2)
    e0, e1, e2 = jnp.exp(l0 - mx), jnp.exp(l1 - mx), jnp.exp(l2 - mx)
    inv = 1.0 / (e0 + e1 + e2)
    w0, w1, w2 = e0 * inv, e1 * inv, e2 * inv

    p1t = _deinterleave_matrix(d1, transpose=True)
    p2t = _deinterleave_matrix(d2, transpose=True)
    for hp in range(B_HEADS // 2):
        pair = (2 * hp, 2 * hp + 1)
        src1 = jnp.concatenate([o1_ref[h].reshape(tm, B_HEAD_DIM) for h in pair], axis=1)
        src2 = jnp.concatenate([o2_ref[h].reshape(tm, B_HEAD_DIM) for h in pair], axis=1)
        t1 = jnp.dot(p1t, src1, preferred_element_type=jnp.float32)
        t2 = jnp.dot(p2t, src2, preferred_element_type=jnp.float32)
        for k, h in enumerate(pair):
            cs = slice(h * B_HEAD_DIM, (h + 1) * B_HEAD_DIM)
            hs = slice(k * B_HEAD_DIM, (k + 1) * B_HEAD_DIM)
            o = (w0[:, h:h + 1] * o0_ref[h].astype(jnp.float32)
                 + w1[:, h:h + 1] * t1[:, hs] + w2[:, h:h + 1] * t2[:, hs])
            gt_new_ref[:, cs] = (o * sz_ref[h].astype(jnp.float32)).astype(jnp.bfloat16)
    gt_ref[...] = gt_new_ref[...]


def _merge_out(os_, lses, qkvz0, w_out_stack, layer, x, next_g, *, final_norm, tm=PERM_BLOCK):
    m, d = x.shape
    assert tm == PERM_BLOCK
    tiles_per_seq = SEQ // tm
    n_tiles = m // tm
    kern = functools.partial(_merge_out_kernel, tm=tm, final_norm=final_norm)
    merge = lambda s: jnp.minimum(s, n_tiles - 1)
    done = lambda s: jnp.maximum(s - 1, 0)
    row = lambda w: pl.BlockSpec((tm, w), lambda s: (done(s), 0))
    z_block = 3

    def o_view(o, dil):
        return o.reshape(B_HEADS, BATCH, dil, SEQ // dil, B_HEAD_DIM)

    def o_spec(dil):
        return pl.BlockSpec((B_HEADS, None, dil, tm // dil, B_HEAD_DIM),
                            lambda s: (0, merge(s) // tiles_per_seq, 0, merge(s) % tiles_per_seq, 0))

    def l_view(l, dil):
        return l.reshape(BATCH, dil, SEQ // dil, LANES)

    def l_spec(dil):
        return pl.BlockSpec((None, dil, tm // dil, LANES),
                            lambda s: (merge(s) // tiles_per_seq, 0, merge(s) % tiles_per_seq, 0))

    out_specs = [row(d)]
    out_shape = [jax.ShapeDtypeStruct((m, d), jnp.float32)]
    if not final_norm:
        out_specs.append(row(d))
        out_shape.append(jax.ShapeDtypeStruct((m, d), jnp.bfloat16))
    d1, d2 = B_DILATIONS[1], B_DILATIONS[2]
    return pl.pallas_call(
        kern,
        grid=(n_tiles + 1,),
        in_specs=[pl.BlockSpec((B_HEADS, tm, B_HEAD_DIM), lambda s: (0, merge(s), 0)),
                  o_spec(d1), o_spec(d2),
                  pl.BlockSpec((tm, LANES), lambda s: (merge(s), 0)), l_spec(d1), l_spec(d2),
                  pl.BlockSpec((B_HEADS, tm, B_HEAD_DIM), lambda s: (z_block, merge(s), 0)),
                  pl.BlockSpec((None, B_WIDTH, d), lambda s: (layer, 0, 0), pipeline_mode=pl.Buffered(1)),
                  row(d),
                  pl.BlockSpec((1, d), lambda s: (0, 0))],
        out_specs=out_specs,
        out_shape=out_shape,
        scratch_shapes=[pltpu.VMEM((tm, LANES), jnp.float32),
                        pltpu.VMEM((tm, LANES), jnp.float32),
                        pltpu.VMEM((tm, B_WIDTH), jnp.bfloat16),
                        pltpu.VMEM((tm, B_WIDTH), jnp.bfloat16)],
        compiler_params=_params("arbitrary"),
        name="merge_out",
    )(os_[0], o_view(os_[1], d1), o_view(os_[2], d2), lses[0], l_view(lses[1], d1), l_view(lses[2], d2),
      qkvz0, w_out_stack, x, next_g)


def kernel(x, a_norm_g, a_w_in, a_ln_g, a_ln_b, a_w_s, a_b_s, a_w_out, b_norm_g, b_w_in, b_w_out, final_norm_g):
    bf16 = jnp.bfloat16
    xm = x.reshape(M_TOKENS, D_MODEL)
    gains = []
    for i in range(DEPTH):
        nxt = i + 1
        if nxt == DEPTH:
            gains.append(final_norm_g)
        else:
            gains.append(a_norm_g[nxt // 2] if nxt % 2 == 0 else b_norm_g[nxt // 2])
    gains = [g.reshape(1, D_MODEL) for g in gains]
    a_w_out_bf, b_w_out_bf = a_w_out.astype(bf16), b_w_out.astype(bf16)
    group_tiles = 3 * B_WIDTH // TN
    z_tiles = [3 * B_NGROUPS * B_WIDTH // TN + k for k in range(B_WIDTH // TN)]

    h = _norm(xm, a_norm_g[0].reshape(1, D_MODEL))
    for i in range(DEPTH):
        j = i // 2
        if i % 2 == 0:
            p = _proj(h, a_w_in, j, list(range(3 * A_WIDTH // TN)), gelu_tiles=2 * A_WIDTH // TN,
                      silu_from=2 * A_WIDTH // TN)
            xm, hs = _gmlp_out(p, a_ln_g[j].reshape(1, A_WIDTH), a_ln_b[j].reshape(1, A_WIDTH),
                               a_w_s[j], a_b_s[j].T, a_w_out_bf, j, xm, gains[i])
        else:
            qkvs = []
            for g in range(B_NGROUPS):
                tiles = list(range(g * group_tiles, (g + 1) * group_tiles)) + (z_tiles if g == 0 else [])
                qkvs.append(_proj(hs[g], b_w_in, j, tiles, silu_from=group_tiles, head_major=True))
            outs = [_attention_group(q, g) for g, q in enumerate(qkvs)]
            res = _merge_out([o for o, _ in outs], [l for _, l in outs], qkvs[0], b_w_out_bf, j,
                             xm, gains[i], final_norm=(i == DEPTH - 1))
            if i == DEPTH - 1:
                xm = res[0]
            else:
                xm, h = res
    return xm.reshape(BATCH, SEQ, D_MODEL)
```
